```python
import math
import jax, jax.numpy as jnp
from jax import lax
import numpy as np

D_MODEL = 1024
BATCH = 2
SEQ = 8192
DEPTH = 2
DEC_BATCH = 128
DEC_SEQ = 8
PAST_LEN = 2048
PAGE_SIZE = 128

MIX_WIDTH = D_MODEL
DIFF_QK_DIM = 64
DIFF_V_DIM = 2 * DIFF_QK_DIM
DIFF_HEADS = (MIX_WIDTH // 2) // DIFF_V_DIM
SB_DIM = 64
SB_HEADS = (MIX_WIDTH // 2) // SB_DIM
FF_DIM = -(-8 * D_MODEL // (3 * 256)) * 256
N_BUCKETS = 32
MAX_DISTANCE = 128
Q_BLOCK = 128
EPS = 1e-6

_PROJ_WIDTHS = (DIFF_HEADS * 2 * DIFF_QK_DIM, DIFF_HEADS * 2 * DIFF_QK_DIM, DIFF_HEADS * DIFF_V_DIM,
                SB_HEADS * SB_DIM, SB_HEADS * SB_DIM, SB_HEADS * SB_DIM)
IN_WIDTH = sum(_PROJ_WIDTHS)
SPLIT_POINTS = tuple(int(v) for v in np.cumsum(_PROJ_WIDTHS)[:-1])

kernel_name = "hybrid_diff_stickbreak_decoder_step"


def rmsnorm(x, g):
    x32 = x.astype(jnp.float32)
    y = x32 * lax.rsqrt(jnp.mean(x32 * x32, axis=-1, keepdims=True) + EPS)
    return (y * g.astype(jnp.float32)).astype(x.dtype)


def rel_bucket(rel):
    n = jnp.maximum(rel, 0)
    max_exact = N_BUCKETS // 2
    nf = jnp.maximum(n, 1).astype(jnp.float32)
    large = max_exact + (jnp.log(nf / max_exact) / math.log(MAX_DISTANCE / max_exact)
                         * (N_BUCKETS - max_exact)).astype(jnp.int32)
    large = jnp.minimum(large, N_BUCKETS - 1)
    return jnp.where(n < max_exact, n, large)


def project(h, w_in):
    b, t = h.shape[0], h.shape[1]
    proj = jnp.einsum("btd,de->bte", h, w_in)
    dq, dk, dv, sq, sk, sv = jnp.split(proj, SPLIT_POINTS, axis=-1)
    dq = dq.reshape(b, t, DIFF_HEADS, 2, DIFF_QK_DIM)
    dk = dk.reshape(b, t, DIFF_HEADS, 2 * DIFF_QK_DIM)
    dv = dv.reshape(b, t, DIFF_HEADS, DIFF_V_DIM)
    sq = sq.reshape(b, t, SB_HEADS, SB_DIM)
    sk = sk.reshape(b, t, SB_HEADS, SB_DIM)
    sv = sv.reshape(b, t, SB_HEADS, SB_DIM)
    return dq, dk, dv, sq, sk, sv


def diff_attention(q, k, v, q_pos, k_pos, lam, rel_bias):
    b, tk = k.shape[0], k.shape[1]
    k = k.reshape(b, tk, DIFF_HEADS, 2, DIFF_QK_DIM)
    s = jnp.einsum("bqhcd,bkhcd->bchqk", q, k, preferred_element_type=jnp.float32) * (DIFF_QK_DIM ** -0.5)
    bias = rel_bias[rel_bucket(q_pos[:, None] - k_pos[None, :])].astype(jnp.float32)
    bias = jnp.transpose(bias, (2, 0, 1))
    mask = k_pos[None, :] <= q_pos[:, None]
    s = jnp.where(mask, s + bias, jnp.finfo(jnp.float32).min)
    p = jax.nn.softmax(s, axis=-1)
    a = p[:, 0] - lam * p[:, 1]
    return jnp.einsum("bhqk,bkhe->bqhe", a, v.astype(jnp.float32))


def stick_breaking(q, k, v, q_pos, k_pos):
    z = jnp.einsum("bqhd,bkhd->bhqk", q, k, preferred_element_type=jnp.float32) * (SB_DIM ** -0.5)
    mask = k_pos[None, :] < q_pos[:, None]
    log_beta = jax.nn.log_sigmoid(z)
    log_1mb = jnp.where(mask, jax.nn.log_sigmoid(-z), 0.0)
    later = lax.cumsum(log_1mb, axis=3, reverse=True) - log_1mb
    w = jnp.where(mask, jnp.exp(log_beta + later), 0.0)
    return jnp.einsum("bhqk,bkhd->bqhd", w, v.astype(jnp.float32))


def merge_heads(od, os_, subln_g, lambda_init, w_o, dtype):
    b, t = od.shape[0], od.shape[1]
    od = rmsnorm(od, subln_g) * (1.0 - lambda_init)
    cat = jnp.concatenate([od.reshape(b, t, -1), os_.reshape(b, t, -1)], axis=-1).astype(dtype)
    return jnp.einsum("bte,ed->btd", cat, w_o)


def layer_tail(x, mix, g, w_gate_up, w_down):
    x = x + rmsnorm(mix, g[1])
    h = rmsnorm(x, g[2])
    gate, up = jnp.split(jnp.einsum("btd,df->btf", h, w_gate_up), 2, axis=-1)
    f = jnp.einsum("btf,fd->btd", jax.nn.silu(gate) * up, w_down)
    return x + rmsnorm(f, g[3])


def setup_inputs(seed: int = 0) -> dict:
    key = jax.random.key(seed)
    ks = jax.random.split(key, 20)
    n_pages = PAST_LEN // PAGE_SIZE
    n_used = DEC_BATCH * n_pages
    n_pool = (n_used * 5) // 4
    f32 = jnp.float32
    perm = jax.random.permutation(ks[0], n_pool)[:n_used]
    page_table = perm.reshape(DEC_BATCH, n_pages).astype(jnp.int32)
    return {
        "x_prompt": jax.random.normal(ks[1], (BATCH, SEQ, D_MODEL), f32),
        "x_sample": jax.random.normal(ks[2], (DEC_BATCH, DEC_SEQ, D_MODEL), f32),
        "cache_diff_k": jax.random.normal(ks[3], (DEPTH, n_pool, PAGE_SIZE, DIFF_HEADS, 2 * DIFF_QK_DIM), f32),
        "cache_diff_v": jax.random.normal(ks[4], (DEPTH, n_pool, PAGE_SIZE, DIFF_HEADS, DIFF_V_DIM), f32),
        "cache_sb_k": jax.random.normal(ks[5], (DEPTH, n_pool, PAGE_SIZE, SB_HEADS, SB_DIM), f32),
        "cache_sb_v": jax.random.normal(ks[6], (DEPTH, n_pool, PAGE_SIZE, SB_HEADS, SB_DIM), f32),
        "page_table": page_table,
        "rel_bias": 0.1 * jax.random.normal(ks[7], (N_BUCKETS, DIFF_HEADS), f32),
        "norm_g": 1.0 + 0.05 * jax.random.normal(ks[8], (DEPTH, 4, D_MODEL), f32),
        "w_in": jax.random.normal(ks[9], (DEPTH, D_MODEL, IN_WIDTH), f32) * D_MODEL ** -0.5,
        "lam_params": 0.1 * jax.random.normal(ks[10], (DEPTH, 4, DIFF_QK_DIM), f32),
        "subln_g": 1.0 + 0.05 * jax.random.normal(ks[11], (DEPTH, DIFF_V_DIM), f32),
        "w_o": jax.random.normal(ks[12], (DEPTH, MIX_WIDTH, D_MODEL), f32) * MIX_WIDTH ** -0.5,
        "w_gate_up": jax.random.normal(ks[13], (DEPTH, D_MODEL, 2 * FF_DIM), f32) * D_MODEL ** -0.5,
        "w_down": jax.random.normal(ks[14], (DEPTH, FF_DIM, D_MODEL), f32) * FF_DIM ** -0.5,
    }


def reference(x_prompt, x_sample, cache_diff_k, cache_diff_v, cache_sb_k, cache_sb_v, page_table,
              rel_bias, norm_g, w_in, lam_params, subln_g, w_o, w_gate_up, w_down):
    xp, xs = x_prompt, x_sample
    b_p, seq = xp.shape[0], xp.shape[1]
    b_s, dec_seq = xs.shape[0], xs.shape[1]
    n_seq, n_pages = page_table.shape
    past_len = n_pages * PAGE_SIZE
    n_blocks = seq // Q_BLOCK
    pos_p = jnp.arange(seq, dtype=jnp.int32)
    q_pos_s = past_len + jnp.arange(dec_seq, dtype=jnp.int32)
    k_pos_s = jnp.arange(past_len + dec_seq, dtype=jnp.int32)
    out_dk_p, out_dv_p, out_sk_p, out_sv_p = [], [], [], []
    out_dk_s, out_dv_s, out_sk_s, out_sv_s = [], [], [], []
    for l in range(DEPTH):
        lambda_init = 0.8 - 0.6 * math.exp(-0.3 * l)
        lp = lam_params[l].astype(jnp.float32)
        lam = jnp.exp(jnp.sum(lp[0] * lp[1])) - jnp.exp(jnp.sum(lp[2] * lp[3])) + lambda_init

        hp = rmsnorm(xp, norm_g[l, 0])
        dq, dk, dv, sq, sk, sv = project(hp, w_in[l])

        def block(i, dq=dq, dk=dk, dv=dv, sq=sq, sk=sk, sv=sv, lam=lam):
            s0 = i * Q_BLOCK
            q_pos = s0 + jnp.arange(Q_BLOCK, dtype=jnp.int32)
            qd = lax.dynamic_slice_in_dim(dq, s0, Q_BLOCK, axis=1)
            qs = lax.dynamic_slice_in_dim(sq, s0, Q_BLOCK, axis=1)
            return (diff_attention(qd, dk, dv, q_pos, pos_p, lam, rel_bias),
                    stick_breaking(qs, sk, sv, q_pos, pos_p))

        od, osb = lax.map(block, jnp.arange(n_blocks, dtype=jnp.int32))
        od = jnp.moveaxis(od, 0, 1).reshape(b_p, seq, DIFF_HEADS, DIFF_V_DIM)
        osb = jnp.moveaxis(osb, 0, 1).reshape(b_p, seq, SB_HEADS, SB_DIM)
        mix = merge_heads(od, osb, subln_g[l], lambda_init, w_o[l], xp.dtype)
        xp = layer_tail(xp, mix, norm_g[l], w_gate_up[l], w_down[l])
        out_dk_p.append(dk); out_dv_p.append(dv); out_sk_p.append(sk); out_sv_p.append(sv)

        hs = rmsnorm(xs, norm_g[l, 0])
        dq_s, dk_s, dv_s, sq_s, sk_s, sv_s = project(hs, w_in[l])
        past_dk = cache_diff_k[l, page_table].reshape(n_seq, past_len, DIFF_HEADS, 2 * DIFF_QK_DIM)
        past_dv = cache_diff_v[l, page_table].reshape(n_seq, past_len, DIFF_HEADS, DIFF_V_DIM)
        past_sk = cache_sb_k[l, page_table].reshape(n_seq, past_len, SB_HEADS, SB_DIM)
        past_sv = cache_sb_v[l, page_table].reshape(n_seq, past_len, SB_HEADS, SB_DIM)
        k_d = jnp.concatenate([past_dk.astype(dk_s.dtype), dk_s], axis=1)
        v_d = jnp.concatenate([past_dv.astype(dv_s.dtype), dv_s], axis=1)
        k_b = jnp.concatenate([past_sk.astype(sk_s.dtype), sk_s], axis=1)
        v_b = jnp.concatenate([past_sv.astype(sv_s.dtype), sv_s], axis=1)
        od_s = diff_attention(dq_s, k_d, v_d, q_pos_s, k_pos_s, lam, rel_bias)
        osb_s = stick_breaking(sq_s, k_b, v_b, q_pos_s, k_pos_s)
        mix_s = merge_heads(od_s, osb_s, subln_g[l], lambda_init, w_o[l], xs.dtype)
        xs = layer_tail(xs, mix_s, norm_g[l], w_gate_up[l], w_down[l])
        out_dk_s.append(dk_s); out_dv_s.append(dv_s); out_sk_s.append(sk_s); out_sv_s.append(sv_s)

    new_diff_k_prompt = jnp.stack(out_dk_p)
    new_diff_v_prompt = jnp.stack(out_dv_p)
    new_sb_k_prompt = jnp.stack(out_sk_p)
    new_sb_v_prompt = jnp.stack(out_sv_p)
    new_diff_k_sample = jnp.stack(out_dk_s)
    new_diff_v_sample = jnp.stack(out_dv_s)
    new_sb_k_sample = jnp.stack(out_sk_s)
    new_sb_v_sample = jnp.stack(out_sv_s)
    return (xp, xs, new_diff_k_prompt, new_diff_v_prompt, new_sb_k_prompt, new_sb_v_prompt,
            new_diff_k_sample, new_diff_v_sample, new_sb_k_sample, new_sb_v_sample)
```

```python
import functools
import math

import jax
import jax.numpy as jnp
from jax import lax
from jax.experimental import pallas as pl
from jax.experimental.pallas import tpu as pltpu

F32 = jnp.float32
BF16 = jnp.bfloat16

DIFF_QK_DIM = 64
DIFF_V_DIM = 128
DIFF_HEADS = 4
SB_DIM = 64
SB_HEADS = 8
SEG = 512
N_BUCKETS = 32
MAX_DISTANCE = 128
PAGE_SIZE = 128
EPS = 1e-6
QK_SCALE = 0.125

LANES = 128
TB = 256
TM_TAIL = 256
FF_CHUNK = 1024
VMEM_LIMIT = 56 * 1024 * 1024

NEG = float(jnp.finfo(jnp.float32).min)
SB_EXIT = 110.0

_NT = (((1,), (1,)), ((), ()))


def _rms(x, g):
    return x * lax.rsqrt(jnp.mean(x * x, axis=-1, keepdims=True) + EPS) * g


def _lam(lam_ref, lambda_init):
    lp = lam_ref[...]
    a = jnp.sum(lp[0:1, :] * lp[1:2, :], axis=1, keepdims=True)
    b = jnp.sum(lp[2:3, :] * lp[3:4, :], axis=1, keepdims=True)
    return jnp.exp(a) - jnp.exp(b) + lambda_init


def _store_head_interleaved(ref, x):
    n = x.shape[0]
    for h in range(DIFF_HEADS):
        ref[pl.ds(h, n, stride=DIFF_HEADS), :] = x[:, h * LANES:(h + 1) * LANES]


def _proj_prompt_body(x_ref, g_ref, w_ref, dk_ref, dv_ref, skT_ref, svT_ref,
                      kd_ref, ks_ref, qdT_ref, qsT_ref, vdT_ref, vsT_ref):
    h = _rms(x_ref[...], g_ref[...]).astype(BF16)

    def seg(i):
        return jnp.dot(h, w_ref[:, i * SEG:(i + 1) * SEG], preferred_element_type=F32)

    qdT_ref[0] = (seg(0) * QK_SCALE).T.astype(BF16)
    dk = seg(1)
    _store_head_interleaved(dk_ref, dk)
    kd_ref[...] = dk.astype(BF16)
    dv = seg(2)
    _store_head_interleaved(dv_ref, dv)
    vdT_ref[0] = dv.T.astype(BF16)
    qsT_ref[0] = (seg(3) * QK_SCALE).T.astype(BF16)
    sk = seg(4)
    skT_ref[...] = sk.T
    ks_ref[...] = sk.astype(BF16)
    svT = seg(5).T
    svT_ref[...] = svT
    vsT_ref[0] = svT.astype(BF16)


def _proj_prompt(x, g, w, batch, seq):
    t, d = x.shape
    nb = t // TB
    nq = seq // TB
    row = lambda i: (i, 0)
    blk = lambda i: (i, 0, 0)
    seq_major = lambda i: (i // nq, 0, i % nq)
    il_out = jax.ShapeDtypeStruct((t * DIFF_HEADS, LANES), F32)
    tr32_out = jax.ShapeDtypeStruct((batch, SEG, seq), F32)
    bf_out = jax.ShapeDtypeStruct((t, SEG), BF16)
    tr_out = jax.ShapeDtypeStruct((nb, SEG, TB), BF16)
    return pl.pallas_call(
        _proj_prompt_body,
        grid=(nb,),
        in_specs=[pl.BlockSpec((TB, d), row),
                  pl.BlockSpec((1, d), lambda i: (0, 0)),
                  pl.BlockSpec((d, 6 * SEG), lambda i: (0, 0), pipeline_mode=pl.Buffered(1))],
        out_specs=([pl.BlockSpec((TB * DIFF_HEADS, LANES), row)] * 2
                   + [pl.BlockSpec((None, SEG, TB), seq_major)] * 2
                   + [pl.BlockSpec((TB, SEG), row)] * 2
                   + [pl.BlockSpec((1, SEG, TB), blk)] * 4),
        out_shape=[il_out] * 2 + [tr32_out] * 2 + [bf_out] * 2 + [tr_out] * 4,
        compiler_params=pltpu.CompilerParams(dimension_semantics=("parallel",),
                                             vmem_limit_bytes=VMEM_LIMIT),
        name="proj_prompt",
    )(x, g, w)


def _proj_sample_body(x_ref, g_ref, w_ref, dk_ref, dv_ref, sk_ref, sv_ref, dq_ref, sq_ref):
    h = _rms(x_ref[...], g_ref[...]).astype(BF16)

    def seg(i):
        return jnp.dot(h, w_ref[:, i * SEG:(i + 1) * SEG], preferred_element_type=F32)

    dq_ref[...] = seg(0) * QK_SCALE
    _store_head_interleaved(dk_ref, seg(1))
    _store_head_interleaved(dv_ref, seg(2))
    sq_ref[...] = seg(3) * QK_SCALE
    sk_ref[...] = seg(4)
    sv_ref[...] = seg(5)


def _proj_sample(x, g, w):
    t, d = x.shape
    row = lambda i: (i, 0)
    il_out = jax.ShapeDtypeStruct((t * DIFF_HEADS, LANES), F32)
    nat_out = jax.ShapeDtypeStruct((t, SEG), F32)
    return pl.pallas_call(
        _proj_sample_body,
        grid=(t // TB,),
        in_specs=[pl.BlockSpec((TB, d), row),
                  pl.BlockSpec((1, d), lambda i: (0, 0)),
                  pl.BlockSpec((d, 6 * SEG), lambda i: (0, 0), pipeline_mode=pl.Buffered(1))],
        out_specs=[pl.BlockSpec((TB * DIFF_HEADS, LANES), row)] * 2 + [pl.BlockSpec((TB, SEG), row)] * 4,
        out_shape=[il_out] * 2 + [nat_out] * 4,
        compiler_params=pltpu.CompilerParams(dimension_semantics=("parallel",),
                                             vmem_limit_bytes=VMEM_LIMIT),
        name="proj_sample",
    )(x, g, w)


def _split_rows(qT):
    row = lax.broadcasted_iota(jnp.int32, qT.shape, 0)
    zero = jnp.zeros_like(qT)
    return jnp.concatenate([jnp.where(row < 64, qT, zero), jnp.where(row >= 64, qT, zero)], axis=1)


def _key_query_iotas():
    key = lax.broadcasted_iota(jnp.int32, (TB, 2 * TB), 0)
    col = lax.broadcasted_iota(jnp.int32, (TB, 2 * TB), 1)
    return key, jnp.where(col >= TB, col - TB, col)


def _diff_body(qT_ref, k_ref, vT_ref, bias_ref, lam_ref, o_ref, acc_ref, *, lambda_init):
    i = pl.program_id(2)
    qcat = _split_rows(qT_ref[0])

    def step(j, m, l, bias, mask):
        k = k_ref[pl.ds(pl.multiple_of(j * TB, TB), TB), :]
        s = jnp.dot(k, qcat, preferred_element_type=F32) + bias
        if mask is not None:
            s = jnp.where(mask, s, NEG)
        m_new = jnp.maximum(m, jnp.max(s, axis=0, keepdims=True))
        alpha = jnp.exp(m - m_new)
        p = jnp.exp(s - m_new)
        l_new = alpha * l + jnp.sum(p, axis=0, keepdims=True)
        pv = jnp.dot(vT_ref[j], p.astype(BF16), preferred_element_type=F32)
        acc_ref[...] = acc_ref[...] * alpha + pv
        return m_new, l_new

    acc_ref[...] = jnp.zeros_like(acc_ref)
    key, query = _key_query_iotas()
    m0 = jnp.full((1, 2 * TB), NEG, F32)
    l0 = jnp.zeros((1, 2 * TB), F32)
    m, l = step(i, m0, l0, bias_ref[0, 0], key <= query)

    def adjacent(ml):
        return step(i - 1, ml[0], ml[1], bias_ref[0, 1], None)

    m, l = lax.cond(i >= 1, adjacent, lambda ml: ml, (m, l))

    def far(j, ml):
        return step(j, ml[0], ml[1], bias_ref[0, 2], None)

    m, l = lax.fori_loop(0, jnp.maximum(i - 1, 0), far, (m, l))

    o = acc_ref[...] * (1.0 / l)
    out = o[:, :TB] - _lam(lam_ref, lambda_init) * o[:, TB:]
    o_ref[...] = out.T


def _diff_attention(qdT, kd, vdT, bias, lam_params, batch, seq, lambda_init):
    nq = seq // TB
    return pl.pallas_call(
        functools.partial(_diff_body, lambda_init=lambda_init),
        grid=(batch, DIFF_HEADS, nq),
        in_specs=[pl.BlockSpec((1, LANES, TB), lambda b, h, i: (b * nq + i, h, 0)),
                  pl.BlockSpec((seq, LANES), lambda b, h, i: (b, h)),
                  pl.BlockSpec((nq, LANES, TB), lambda b, h, i: (b, h, 0)),
                  pl.BlockSpec((1, 3, TB, 2 * TB), lambda b, h, i: (h, 0, 0, 0)),
                  pl.BlockSpec((4, DIFF_QK_DIM), lambda b, h, i: (0, 0))],
        out_specs=pl.BlockSpec((TB, LANES), lambda b, h, i: (b * nq + i, h)),
        out_shape=jax.ShapeDtypeStruct((batch * seq, SEG), F32),
        scratch_shapes=[pltpu.VMEM((LANES, 2 * TB), F32)],
        compiler_params=pltpu.CompilerParams(
            dimension_semantics=("parallel", "parallel", "parallel"),
            vmem_limit_bytes=VMEM_LIMIT),
        name="diff_attention",
    )(qdT, kd, vdT, bias, lam_params)


def _softplus_parts(z):
    sp = jnp.maximum(z, 0.0) + jnp.log(1.0 + jnp.exp(-jnp.abs(z)))
    return sp, z - sp


def _split_bf16(x):
    hi = x.astype(BF16)
    return hi, (x - hi.astype(F32)).astype(BF16)


def _sb_body(qT_ref, k_ref, vT_ref, o_ref, acc_ref, carry_ref):
    i = pl.program_id(2)
    qcat = _split_rows(qT_ref[0])
    r = lax.broadcasted_iota(jnp.int32, (TB, TB), 0)
    c = lax.broadcasted_iota(jnp.int32, (TB, TB), 1)
    later = jnp.where(c > r, 1.0, 0.0).astype(BF16)

    def step(j, carry, mask):
        k = k_ref[pl.ds(pl.multiple_of(j * TB, TB), TB), :]
        z = jnp.dot(k, qcat, preferred_element_type=F32)
        sp, lb = _softplus_parts(z)
        if mask is not None:
            sp = jnp.where(mask, sp, 0.0)
        hi, lo = _split_bf16(sp)
        cum = (jnp.dot(later, hi, preferred_element_type=F32)
               + jnp.dot(later, lo, preferred_element_type=F32))
        w = jnp.exp(lb - cum - carry)
        if mask is not None:
            w = jnp.where(mask, w, 0.0)
        pv = jnp.dot(vT_ref[j], w.astype(BF16), preferred_element_type=F32)
        return pv, carry + cum[0:1, :] + sp[0:1, :]

    key, query = _key_query_iotas()
    pv, carry = step(i, jnp.zeros((1, 2 * TB), F32), key < query)
    acc_ref[...] = pv
    carry_ref[...] = carry

    def more(state):
        j, go = state
        return jnp.logical_and(j >= 0, go > 0)

    def body(state):
        j, _ = state
        pv, carry = step(j, carry_ref[...], None)
        acc_ref[...] += pv
        carry_ref[...] = carry
        return j - 1, (jnp.min(carry) < SB_EXIT).astype(jnp.int32)

    lax.while_loop(more, body, (i - 1, (jnp.min(carry) < SB_EXIT).astype(jnp.int32)))

    acc = acc_ref[...]
    row = lax.broadcasted_iota(jnp.int32, (LANES, TB), 0)
    o_ref[...] = jnp.where(row < SB_DIM, acc[:, :TB], acc[:, TB:]).T


def _sb_attention(qsT, ks, vsT, batch, seq):
    nq = seq // TB
    return pl.pallas_call(
        _sb_body,
        grid=(batch, SB_HEADS // 2, nq),
        in_specs=[pl.BlockSpec((1, LANES, TB), lambda b, h, i: (b * nq + i, h, 0)),
                  pl.BlockSpec((seq, LANES), lambda b, h, i: (b, h)),
                  pl.BlockSpec((nq, LANES, TB), lambda b, h, i: (b, h, 0))],
        out_specs=pl.BlockSpec((TB, LANES), lambda b, h, i: (b * nq + i, h)),
        out_shape=jax.ShapeDtypeStruct((batch * seq, SEG), F32),
        scratch_shapes=[pltpu.VMEM((LANES, 2 * TB), F32), pltpu.VMEM((1, 2 * TB), F32)],
        compiler_params=pltpu.CompilerParams(
            dimension_semantics=("parallel", "parallel", "parallel"),
            vmem_limit_bytes=VMEM_LIMIT),
        name="sb_attention",
    )(qsT, ks, vsT)


DEC_ROWS = 64
DIFF_PAGE_ROWS = PAGE_SIZE * DIFF_HEADS


def _decode_body(pt_ref, qd_ref, qs_ref, kdn_ref, vdn_ref, ksn_ref, vsn_ref,
                 ckd_ref, cvd_ref, cks_ref, cvs_ref, bias_ref, lam_ref,
                 od_ref, os_ref,
                 qfd_ref, qbs_ref, m_ref, l_ref, accd_ref, carry_ref, accs_ref, padd_ref, pads_ref,
                 *, n_pages, lambda_init):
    del pt_ref
    p = pl.program_id(1)
    nq = qd_ref.shape[0]
    r = lax.broadcasted_iota(jnp.int32, (PAGE_SIZE, PAGE_SIZE), 0)
    c = lax.broadcasted_iota(jnp.int32, (PAGE_SIZE, PAGE_SIZE), 1)
    later = jnp.where(r > c, 1.0, 0.0).astype(BF16)

    def diff_block(k, v, bias, new_rows):
        row = lax.broadcasted_iota(jnp.int32, (DEC_ROWS, DIFF_PAGE_ROWS), 0)
        col = lax.broadcasted_iota(jnp.int32, (DEC_ROWS, DIFF_PAGE_ROWS), 1)
        valid = (row // (2 * nq)) == (col % DIFF_HEADS)
        s = lax.dot_general(qfd_ref[...], k.astype(BF16), _NT, preferred_element_type=F32) + bias
        s = jnp.where(valid, s, NEG)
        if new_rows:
            s = jnp.where((col // DIFF_HEADS) <= (row % nq), s, NEG)
        m_old = m_ref[...]
        m_new = jnp.maximum(m_old, jnp.max(s, axis=1, keepdims=True))
        alpha = jnp.exp(m_old - m_new)
        pr = jnp.exp(s - m_new)
        l_ref[...] = alpha * l_ref[...] + jnp.sum(pr, axis=1, keepdims=True)
        m_ref[...] = m_new
        accd_ref[...] = accd_ref[...] * alpha + jnp.dot(pr.astype(BF16), v.astype(BF16),
                                                        preferred_element_type=F32)

    def sb_weights(z, mask):
        sp, lb = _softplus_parts(z)
        if mask is not None:
            sp = jnp.where(mask, sp, 0.0)
        hi, lo = _split_bf16(sp)
        cum = (jnp.dot(hi, later, preferred_element_type=F32)
               + jnp.dot(lo, later, preferred_element_type=F32))
        w = jnp.exp(lb - cum - carry_ref[...])
        if mask is not None:
            w = jnp.where(mask, w, 0.0)
        carry_ref[...] += cum[:, 0:1] + sp[:, 0:1]
        return w.astype(BF16)

    @pl.when(p == 0)
    def _new_rows():
        qd = qd_ref[...]
        lane = lax.broadcasted_iota(jnp.int32, (nq, LANES), 1)
        parts = []
        for h in range(DIFF_HEADS):
            a = qd[:, h * LANES:(h + 1) * LANES]
            parts += [jnp.where(lane < DIFF_QK_DIM, a, 0.0), jnp.where(lane >= DIFF_QK_DIM, a, 0.0)]
        qfd_ref[...] = jnp.concatenate(parts, axis=0).astype(BF16)
        qt = jnp.concatenate([qs_ref[...]] * SB_HEADS, axis=0)
        rg = lax.broadcasted_iota(jnp.int32, qt.shape, 0) // nq
        cg = lax.broadcasted_iota(jnp.int32, qt.shape, 1) // SB_DIM
        qbs_ref[...] = jnp.where(rg == cg, qt, 0.0).astype(BF16)
        m_ref[...] = jnp.full_like(m_ref, NEG)
        l_ref[...] = jnp.zeros_like(l_ref)
        accd_ref[...] = jnp.zeros_like(accd_ref)
        carry_ref[...] = jnp.zeros_like(carry_ref)
        accs_ref[...] = jnp.zeros_like(accs_ref)
        padd_ref[...] = jnp.zeros_like(padd_ref)
        padd_ref[0, 0:nq * DIFF_HEADS, :] = kdn_ref[...]
        padd_ref[1, 0:nq * DIFF_HEADS, :] = vdn_ref[...]
        diff_block(padd_ref[0], padd_ref[1], bias_ref[n_pages], True)
        pads_ref[...] = jnp.zeros_like(pads_ref)
        pads_ref[0, 0:nq, :] = ksn_ref[...]
        pads_ref[1, 0:nq, :] = vsn_ref[...]
        key = lax.broadcasted_iota(jnp.int32, (DEC_ROWS, PAGE_SIZE), 1)
        qi = lax.broadcasted_iota(jnp.int32, (DEC_ROWS, PAGE_SIZE), 0) % nq
        z = lax.dot_general(qbs_ref[...], pads_ref[0].astype(BF16), _NT, preferred_element_type=F32)
        w = sb_weights(z, key < qi)
        accs_ref[...] += jnp.dot(w, pads_ref[1].astype(BF16), preferred_element_type=F32)

    @pl.when(p > 0)
    def _page():
        diff_block(ckd_ref[...], cvd_ref[...], bias_ref[n_pages - p], False)
        z = jnp.dot(qbs_ref[...], cks_ref[...].astype(BF16), preferred_element_type=F32)
        w = sb_weights(z, None)
        accs_ref[...] += lax.dot_general(w, cvs_ref[...].astype(BF16), _NT,
                                         preferred_element_type=F32)

    @pl.when(p == n_pages)
    def _finish():
        lam = _lam(lam_ref, lambda_init)
        o = accd_ref[...] * (1.0 / l_ref[...])
        for h in range(DIFF_HEADS):
            r0 = 2 * h * nq
            od_ref[:, h * LANES:(h + 1) * LANES] = o[r0:r0 + nq, :] - lam * o[r0 + nq:r0 + 2 * nq, :]
        accs = accs_ref[...]
        lane_head = lax.broadcasted_iota(jnp.int32, (nq, SEG), 1) // SB_DIM
        out = jnp.zeros((nq, SEG), F32)
        for h in range(SB_HEADS):
            out = out + jnp.where(lane_head == h, accs[h * nq:(h + 1) * nq, :], 0.0)
        os_ref[...] = out


def _decode_attention(page_table, dq, sq, dk, dv, sk, sv, ckd, cvd, cks, cvs, bias, lam_params,
                      layer, n_seq, dec_seq, lambda_init):
    n_pages = page_table.shape[1]
    assert 2 * DIFF_HEADS * dec_seq == DEC_ROWS and SB_HEADS * dec_seq == DEC_ROWS
    new = pl.BlockSpec((dec_seq, SEG), lambda s, p, pt: (s, 0))
    new_il = pl.BlockSpec((dec_seq * DIFF_HEADS, LANES), lambda s, p, pt: (s, 0))

    def page(s, p, pt):
        return (layer, pt[s, jnp.minimum(n_pages - p, n_pages - 1)], 0, 0)

    cache = pl.BlockSpec((None, None, DIFF_PAGE_ROWS, LANES), page)
    grid_spec = pltpu.PrefetchScalarGridSpec(
        num_scalar_prefetch=1,
        grid=(n_seq, n_pages + 1),
        in_specs=[new, new, new_il, new_il, new, new] + [cache] * 4 + [
            pl.BlockSpec((n_pages + 1, DEC_ROWS, DIFF_PAGE_ROWS), lambda s, p, pt: (0, 0, 0)),
            pl.BlockSpec((4, DIFF_QK_DIM), lambda s, p, pt: (0, 0))],
        out_specs=[new, new],
        scratch_shapes=[pltpu.VMEM((DEC_ROWS, LANES), BF16), pltpu.VMEM((DEC_ROWS, SEG), BF16),
                        pltpu.VMEM((DEC_ROWS, 1), F32), pltpu.VMEM((DEC_ROWS, 1), F32),
                        pltpu.VMEM((DEC_ROWS, LANES), F32),
                        pltpu.VMEM((DEC_ROWS, 1), F32), pltpu.VMEM((DEC_ROWS, SEG), F32),
                        pltpu.VMEM((2, DIFF_PAGE_ROWS, LANES), F32),
                        pltpu.VMEM((2, PAGE_SIZE, SEG), F32)])
    return pl.pallas_call(
        functools.partial(_decode_body, n_pages=n_pages, lambda_init=lambda_init),
        grid_spec=grid_spec,
        out_shape=[jax.ShapeDtypeStruct((n_seq * dec_seq, SEG), F32)] * 2,
        compiler_params=pltpu.CompilerParams(dimension_semantics=("parallel", "arbitrary"),
                                             vmem_limit_bytes=VMEM_LIMIT),
        name="decode_attention",
    )(page_table, dq, sq, dk, dv, sk, sv, ckd, cvd, cks, cvs, bias, lam_params)


def _tail_body(x_ref, od_ref, os_ref, subg_ref, g_ref, wo_ref, wgu_ref, wdn_ref, out_ref,
               *, lambda_init):
    od = od_ref[...]
    subg = subg_ref[...]
    parts = []
    for h in range(DIFF_HEADS):
        a = od[:, h * DIFF_V_DIM:(h + 1) * DIFF_V_DIM]
        parts.append(_rms(a, subg) * (1.0 - lambda_init))
    cat = jnp.concatenate(parts + [os_ref[...]], axis=-1).astype(BF16)
    mix = jnp.dot(cat, wo_ref[...], preferred_element_type=F32)
    x1 = x_ref[...] + _rms(mix, g_ref[1:2, :])
    h = _rms(x1, g_ref[2:3, :]).astype(BF16)
    ff = wdn_ref.shape[0]
    f = jnp.zeros(x1.shape, F32)
    for c0 in range(0, ff, FF_CHUNK):
        c1 = min(c0 + FF_CHUNK, ff)
        gate = jnp.dot(h, wgu_ref[:, c0:c1], preferred_element_type=F32)
        up = jnp.dot(h, wgu_ref[:, ff + c0:ff + c1], preferred_element_type=F32)
        act = gate * (1.0 / (1.0 + jnp.exp(-gate))) * up
        f = f + jnp.dot(act.astype(BF16), wdn_ref[c0:c1, :], preferred_element_type=F32)
    out_ref[...] = x1 + _rms(f, g_ref[3:4, :])


def _tail(x, od, os_, subg, g, wo, wgu, wdn, lambda_init):
    t, d = x.shape
    ff = wdn.shape[0]
    row = lambda i: (i, 0)
    const = lambda i: (0, 0)
    resident = functools.partial(pl.BlockSpec, index_map=const, pipeline_mode=pl.Buffered(1))
    return pl.pallas_call(
        functools.partial(_tail_body, lambda_init=lambda_init),
        grid=(t // TM_TAIL,),
        in_specs=[pl.BlockSpec((TM_TAIL, d), row),
                  pl.BlockSpec((TM_TAIL, SEG), row),
                  pl.BlockSpec((TM_TAIL, SEG), row),
                  pl.BlockSpec((1, DIFF_V_DIM), const),
                  pl.BlockSpec((4, d), const),
                  resident((d, d)), resident((d, 2 * ff)), resident((ff, d))],
        out_specs=pl.BlockSpec((TM_TAIL, d), row),
        out_shape=jax.ShapeDtypeStruct((t, d), F32),
        compiler_params=pltpu.CompilerParams(dimension_semantics=("parallel",),
                                             vmem_limit_bytes=VMEM_LIMIT),
        name="merge_ffn",
    )(x, od, os_, subg, g, wo, wgu, wdn)


def _rel_bucket(rel):
    n = jnp.maximum(rel, 0)
    max_exact = N_BUCKETS // 2
    nf = jnp.maximum(n, 1).astype(F32)
    large = max_exact + (jnp.log(nf / max_exact) / math.log(MAX_DISTANCE / max_exact)
                         * (N_BUCKETS - max_exact)).astype(jnp.int32)
    large = jnp.minimum(large, N_BUCKETS - 1)
    return jnp.where(n < max_exact, n, large)


def _prompt_bias_tiles(rel_bias):
    assert TB + 1 >= MAX_DISTANCE
    key = jnp.arange(TB, dtype=jnp.int32)[:, None]
    query = jnp.arange(TB, dtype=jnp.int32)[None, :]
    dist = jnp.stack([query - key, TB + query - key, jnp.full((TB, TB), 2 * TB, jnp.int32)])
    tiles = jnp.transpose(rel_bias[_rel_bucket(dist)].astype(F32), (3, 0, 1, 2))
    return jnp.concatenate([tiles, tiles], axis=-1)


def _decode_bias_tiles(rel_bias, n_pages, dec_seq):
    past = n_pages * PAGE_SIZE
    qi = jnp.arange(dec_seq, dtype=jnp.int32)[None, :, None]
    key = jnp.arange(PAGE_SIZE, dtype=jnp.int32)[None, None, :]
    page = jnp.arange(n_pages, dtype=jnp.int32)[:, None, None]
    dist = jnp.concatenate([past + qi - (page * PAGE_SIZE + key), qi - key], axis=0)
    b = rel_bias[_rel_bucket(dist)].astype(F32)
    b = jnp.transpose(b, (0, 3, 1, 2))[:, :, None, :, :, None]
    b = jnp.broadcast_to(b, (n_pages + 1, DIFF_HEADS, 2, dec_seq, PAGE_SIZE, DIFF_HEADS))
    return b.reshape(n_pages + 1, DEC_ROWS, DIFF_PAGE_ROWS)


def kernel(x_prompt, x_sample, cache_diff_k, cache_diff_v, cache_sb_k, cache_sb_v, page_table,
           rel_bias, norm_g, w_in, lam_params, subln_g, w_o, w_gate_up, w_down):
    batch, seq, d = x_prompt.shape
    n_seq, dec_seq, _ = x_sample.shape
    depth = w_in.shape[0]
    n_pages = page_table.shape[1]
    n_pool = cache_diff_k.shape[1]

    xp = x_prompt.reshape(batch * seq, d)
    xs = x_sample.reshape(n_seq * dec_seq, d)
    ckd = cache_diff_k.reshape(depth, n_pool, DIFF_PAGE_ROWS, LANES)
    cvd = cache_diff_v.reshape(depth, n_pool, DIFF_PAGE_ROWS, LANES)
    cks = jnp.transpose(cache_sb_k, (0, 1, 3, 4, 2)).reshape(depth, n_pool, SEG, PAGE_SIZE)
    cvs = jnp.transpose(cache_sb_v, (0, 1, 3, 4, 2)).reshape(depth, n_pool, SEG, PAGE_SIZE)
    bias_p = _prompt_bias_tiles(rel_bias)
    bias_s = _decode_bias_tiles(rel_bias, n_pages, dec_seq)

    new_p, new_s = [], []
    for l in range(depth):
        lambda_init = 0.8 - 0.6 * math.exp(-0.3 * l)
        g = norm_g[l]
        w_in_l = w_in[l].astype(BF16)
        wo_l = w_o[l].astype(BF16)
        wgu_l = w_gate_up[l].astype(BF16)
        wdn_l = w_down[l].astype(BF16)
        subg = subln_g[l][None, :]
        lam_l = lam_params[l]

        dk, dv, skT, svT, kd, ks, qdT, qsT, vdT, vsT = _proj_prompt(xp, g[0:1], w_in_l, batch, seq)
        od = _diff_attention(qdT, kd, vdT, bias_p, lam_l, batch, seq, lambda_init)
        os_ = _sb_attention(qsT, ks, vsT, batch, seq)
        xp = _tail(xp, od, os_, subg, g, wo_l, wgu_l, wdn_l, lambda_init)
        new_p.append((dk, dv, skT, svT))

        dk_s, dv_s, sk_s, sv_s, dq_s, sq_s = _proj_sample(xs, g[0:1], w_in_l)
        od_s, os_s = _decode_attention(page_table, dq_s, sq_s, dk_s, dv_s, sk_s, sv_s,
                                       ckd, cvd, cks, cvs, bias_s, lam_l, l, n_seq, dec_seq, lambda_init)
        xs = _tail(xs, od_s, os_s, subg, g, wo_l, wgu_l, wdn_l, lambda_init)
        new_s.append((dk_s, dv_s, sk_s, sv_s))

    def stack(parts, idx):
        return jnp.stack([p[idx] for p in parts])

    def diff_leaf(x, lead):
        return x.reshape(depth, *lead, DIFF_HEADS, DIFF_V_DIM)

    def sb_leaf_t(x):
        return jnp.transpose(x.reshape(depth, batch, SB_HEADS, SB_DIM, seq), (0, 1, 4, 2, 3))

    def sb_leaf(x):
        return x.reshape(depth, n_seq, dec_seq, SB_HEADS, SB_DIM)

    return (xp.reshape(batch, seq, d), xs.reshape(n_seq, dec_seq, d),
            diff_leaf(stack(new_p, 0), (batch, seq)), diff_leaf(stack(new_p, 1), (batch, seq)),
            sb_leaf_t(stack(new_p, 2)), sb_leaf_t(stack(new_p, 3)),
            diff_leaf(stack(new_s, 0), (n_seq, dec_seq)), diff_leaf(stack(new_s, 1), (n_seq, dec_seq)),
            sb_leaf(stack(new_s, 2)), sb_leaf(stack(new_s, 3)))
```
